```python
import math
import jax
import jax.numpy as jnp
from jax import lax
import numpy as np


D_MODEL = 1024
BATCH = 8
SEQ = 4096
DEPTH = 1
DEC_BATCH = 128
DEC_SEQ = 8
PAST_LEN = 8192
PAGE_SIZE = 128

HGRN_HEADS = 4
HGRN_DK = 128
HGRN_DV = 128
HGRN_CHUNK = 64
ATTN_HEADS = 8
HEAD_DIM = 64
IDX_HEADS = 8
IDX_DIM = 64
MAX_TOPK = 256
Q_BLOCK = 128
D_FF = 2816
PLE_DIM = 256
ROPE_THETA = 10000.0
NORM_EPS = 1e-6
COL_WIDTHS = (HGRN_HEADS * HGRN_DK, HGRN_HEADS * HGRN_DK, HGRN_HEADS * HGRN_DV, HGRN_HEADS * HGRN_DV,
              ATTN_HEADS * HEAD_DIM, HEAD_DIM, HEAD_DIM,
              IDX_HEADS * IDX_DIM, IDX_DIM, IDX_HEADS,
              D_MODEL, D_MODEL)
N_IN = sum(COL_WIDTHS)

kernel_name = 'hgrn2_dsa_gated_hybrid_step'


def split_columns(z):
    points, acc = [], 0
    for w in COL_WIDTHS[:-1]:
        acc += w
        points.append(acc)
    return jnp.split(z, points, axis=-1)


def rmsnorm(x, g):
    xf = x.astype(jnp.float32)
    y = xf * lax.rsqrt(jnp.mean(xf * xf, axis=-1, keepdims=True) + NORM_EPS)
    return (y * g.astype(jnp.float32)).astype(x.dtype)


def swiglu(x, w_gate, w_up, w_down):
    return (jax.nn.silu(x @ w_gate) * (x @ w_up)) @ w_down


def rope(x, pos):
    half = x.shape[-1] // 2
    inv = ROPE_THETA ** (-jnp.arange(half, dtype=jnp.float32) / half)
    ang = pos.astype(jnp.float32)[:, None] * inv[None, :]
    cos = jnp.cos(ang)[None, :, None, :]
    sin = jnp.sin(ang)[None, :, None, :]
    xf = x.astype(jnp.float32)
    x1, x2 = xf[..., :half], xf[..., half:]
    return jnp.concatenate([x1 * cos - x2 * sin, x1 * sin + x2 * cos], axis=-1).astype(x.dtype)


def gather_rows(rows, idx):
    return jax.vmap(lambda r, i: r[i])(rows, idx)


def hgrn2_recurrence(q, k, v, log_f, s0, chunk):
    b, t, h, _ = q.shape
    dv = v.shape[-1]
    n_chunks = t // chunk

    def to_chunks(a):
        return a.reshape(b, n_chunks, chunk, h, a.shape[-1]).swapaxes(0, 1)

    causal = jnp.tril(jnp.ones((chunk, chunk), dtype=bool))[None, :, :, None, None]

    def step(s, xs):
        qc, kc, vc, lfc = xs
        g = jnp.cumsum(lfc, axis=1)
        decay = jnp.exp(jnp.where(causal, g[:, :, None] - g[:, None, :], -jnp.inf))
        a = jnp.einsum('bthd,btshd,bshd->bhts', qc, decay, kc)
        o = (jnp.einsum('bhts,bshv->bthv', a, vc)
             + jnp.einsum('bthd,bhdv->bthv', qc * jnp.exp(g), s))
        g_last = g[:, -1]
        s = (jnp.exp(g_last)[..., None] * s
             + jnp.einsum('bshd,bshv->bhdv', kc * jnp.exp(g_last[:, None] - g), vc))
        return s, o

    s, o = lax.scan(step, s0, (to_chunks(q), to_chunks(k), to_chunks(v), to_chunks(log_f)))
    return o.swapaxes(0, 1).reshape(b, t, h, dv), s


def indexer_select(iq, iw, ik, q_pos, topk):
    s = jnp.einsum('bthd,bld->bthl', iq, ik)
    score = jnp.einsum('bth,bthl->btl', iw.astype(jnp.float32), jax.nn.relu(s).astype(jnp.float32))
    mask = jnp.arange(ik.shape[1])[None, :] <= q_pos[:, None]
    score = jnp.where(mask[None], score, -jnp.inf)
    _, idx = lax.top_k(score, topk)
    valid = idx <= q_pos[None, :, None]
    return idx, valid


def sparse_attend(q, k_sel, v_sel, valid):
    logits = jnp.einsum('bthd,btkd->bthk', q, k_sel).astype(jnp.float32) * (q.shape[-1] ** -0.5)
    logits = jnp.where(valid[:, :, None, :], logits, -jnp.inf)
    p = jax.nn.softmax(logits, axis=-1)
    return jnp.einsum('bthk,btkd->bthd', p.astype(v_sel.dtype), v_sel)


def prompt_attention(q, k, v, iq, ik, iw):
    b, s = q.shape[:2]
    topk = min(MAX_TOPK, s // 4)
    n_blocks = s // Q_BLOCK

    def blocks(a):
        return a.reshape(b, n_blocks, Q_BLOCK, *a.shape[2:]).swapaxes(0, 1)

    def one_block(xs):
        qb, iqb, iwb, t0 = xs
        q_pos = t0 + jnp.arange(Q_BLOCK)
        idx, valid = indexer_select(iqb, iwb, ik, q_pos, topk)
        return sparse_attend(qb, gather_rows(k, idx), gather_rows(v, idx), valid)

    out = lax.map(one_block, (blocks(q), blocks(iq), blocks(iw), jnp.arange(n_blocks) * Q_BLOCK))
    return out.swapaxes(0, 1).reshape(q.shape)


def sample_attention(q, k_new, v_new, iq, ik_new, iw, ck, cv, cik, page_table):
    bd, t = q.shape[:2]
    past_len = page_table.shape[1] * PAGE_SIZE
    topk = min(MAX_TOPK, (past_len + t) // 4)
    ik_past = cik[page_table].reshape(bd, past_len, cik.shape[-1])
    ik_all = jnp.concatenate([ik_past, ik_new], axis=1)
    q_pos = past_len + jnp.arange(t)
    idx, valid = indexer_select(iq, iw, ik_all, q_pos, topk)
    in_past = (idx < past_len)[..., None]
    pidx = jnp.minimum(idx, past_len - 1)
    phys = jax.vmap(lambda pt, i: pt[i])(page_table, pidx // PAGE_SIZE)
    off = pidx % PAGE_SIZE
    nidx = jnp.clip(idx - past_len, 0, t - 1)
    k_sel = jnp.where(in_past, ck[phys, off], gather_rows(k_new, nidx))
    v_sel = jnp.where(in_past, cv[phys, off], gather_rows(v_new, nidx))
    return sparse_attend(q, k_sel, v_sel, valid)


def trunk(x, p, pos, hgrn_mix, attn_mix, prm):
    b, t = x.shape[:2]
    lb_all = jnp.cumsum(jax.nn.softmax(prm['hgrn_lb_logits'].astype(jnp.float32), axis=0), axis=0)
    k_rows, v_rows, ik_rows, states = [], [], [], []
    for i in range(DEPTH):
        x = x + 0.5 * swiglu(rmsnorm(x, prm['g_ffn1'][i]), prm['w1_gate'][i], prm['w1_up'][i], prm['w1_down'][i])
        h = rmsnorm(x, prm['g_mix'][i])
        z = h @ prm['w_in'][i]
        (zq_a, zf_a, zi_a, zg_a, zq_b, zk_b, zv_b, zq_i, zk_i, zw_i, zgate_a, zgate_b) = split_columns(z)
        lb = lb_all[i].reshape(HGRN_HEADS, HGRN_DK)
        sf = jax.nn.sigmoid(zf_a.astype(jnp.float32).reshape(b, t, HGRN_HEADS, HGRN_DK))
        f = lb + (1.0 - lb) * sf
        q_a = jax.nn.silu(zq_a.astype(jnp.float32)).reshape(b, t, HGRN_HEADS, HGRN_DK)
        v_a = zi_a.astype(jnp.float32).reshape(b, t, HGRN_HEADS, HGRN_DV)
        o_a, s_a = hgrn_mix(i, q_a, 1.0 - f, v_a, jnp.log(f))
        o_a = rmsnorm(o_a, prm['g_hgrn_out'][i].reshape(HGRN_HEADS, HGRN_DV))
        o_a = (o_a * jax.nn.silu(zg_a.astype(jnp.float32)).reshape(b, t, HGRN_HEADS, HGRN_DV))
        branch_a = o_a.reshape(b, t, HGRN_HEADS * HGRN_DV).astype(x.dtype) @ prm['w_branch_a'][i]
        q_b = rope(zq_b.reshape(b, t, ATTN_HEADS, HEAD_DIM), pos)
        k_b = rope(zk_b.reshape(b, t, 1, HEAD_DIM), pos)[:, :, 0]
        v_b = zv_b
        iq = rope(zq_i.reshape(b, t, IDX_HEADS, IDX_DIM), pos)
        ik = rope(zk_i.reshape(b, t, 1, IDX_DIM), pos)[:, :, 0]
        iw = zw_i * ((IDX_HEADS ** -0.5) * (IDX_DIM ** -0.5))
        o_b = attn_mix(i, q_b, k_b, v_b, iq, ik, iw).reshape(b, t, ATTN_HEADS * HEAD_DIM)
        branch_b = o_b @ prm['w_branch_b'][i]
        merged = jax.nn.sigmoid(zgate_a) * branch_a + jax.nn.sigmoid(zgate_b) * branch_b
        x = x + merged @ prm['w_out'][i]
        x = x + 0.5 * swiglu(rmsnorm(x, prm['g_ffn2'][i]), prm['w2_gate'][i], prm['w2_up'][i], prm['w2_down'][i])
        gate = jax.nn.sigmoid(rmsnorm(x, prm['g_ple'][i]) @ prm['w_ple_gate'][i])
        x = x + gate * (p[i] @ prm['w_ple'][i])
        k_rows.append(k_b)
        v_rows.append(v_b)
        ik_rows.append(ik)
        states.append(s_a.astype(x.dtype))
    y = rmsnorm(x, prm['g_final'])
    return y, jnp.stack(k_rows), jnp.stack(v_rows), jnp.stack(ik_rows), jnp.stack(states)


def setup_inputs(seed: int = 0) -> dict:
    key = jax.random.key(seed)
    ks = jax.random.split(key, 40)
    f32 = jnp.float32

    def nrm(k, shape, scale=1.0):
        return jax.random.normal(k, shape, f32) * scale

    n_pages = PAST_LEN // PAGE_SIZE
    used = DEC_BATCH * n_pages
    n_pool = used + max(1, used // 4)
    page_table = jax.random.permutation(ks[0], n_pool)[:used].reshape(DEC_BATCH, n_pages).astype(jnp.int32)
    hd = HGRN_HEADS * HGRN_DV
    ad = ATTN_HEADS * HEAD_DIM
    return {
        'x_prompt': nrm(ks[1], (BATCH, SEQ, D_MODEL)),
        'x_sample': nrm(ks[2], (DEC_BATCH, DEC_SEQ, D_MODEL)),
        'cache_k': nrm(ks[3], (DEPTH, n_pool, PAGE_SIZE, HEAD_DIM)),
        'cache_v': nrm(ks[4], (DEPTH, n_pool, PAGE_SIZE, HEAD_DIM)),
        'cache_idx_k': nrm(ks[5], (DEPTH, n_pool, PAGE_SIZE, IDX_DIM)),
        'state_hgrn': nrm(ks[6], (DEPTH, DEC_BATCH, HGRN_HEADS, HGRN_DK, HGRN_DV), 0.5),
        'page_table': page_table,
        'p_prompt': nrm(ks[7], (DEPTH, BATCH, SEQ, PLE_DIM)),
        'p_sample': nrm(ks[8], (DEPTH, DEC_BATCH, DEC_SEQ, PLE_DIM)),
        'g_ffn1': 1.0 + nrm(ks[9], (DEPTH, D_MODEL), 0.02),
        'w1_gate': nrm(ks[10], (DEPTH, D_MODEL, D_FF), D_MODEL ** -0.5),
        'w1_up': nrm(ks[11], (DEPTH, D_MODEL, D_FF), D_MODEL ** -0.5),
        'w1_down': nrm(ks[12], (DEPTH, D_FF, D_MODEL), D_FF ** -0.5),
        'g_mix': 1.0 + nrm(ks[13], (DEPTH, D_MODEL), 0.02),
        'w_in': nrm(ks[14], (DEPTH, D_MODEL, N_IN), D_MODEL ** -0.5),
        'hgrn_lb_logits': nrm(ks[15], (DEPTH + 1, HGRN_HEADS * HGRN_DK), 0.1),
        'g_hgrn_out': 1.0 + nrm(ks[16], (DEPTH, hd), 0.02),
        'w_branch_a': nrm(ks[17], (DEPTH, hd, D_MODEL), hd ** -0.5),
        'w_branch_b': nrm(ks[18], (DEPTH, ad, D_MODEL), ad ** -0.5),
        'w_out': nrm(ks[19], (DEPTH, D_MODEL, D_MODEL), D_MODEL ** -0.5),
        'g_ffn2': 1.0 + nrm(ks[20], (DEPTH, D_MODEL), 0.02),
        'w2_gate': nrm(ks[21], (DEPTH, D_MODEL, D_FF), D_MODEL ** -0.5),
        'w2_up': nrm(ks[22], (DEPTH, D_MODEL, D_FF), D_MODEL ** -0.5),
        'w2_down': nrm(ks[23], (DEPTH, D_FF, D_MODEL), D_FF ** -0.5),
        'g_ple': 1.0 + nrm(ks[24], (DEPTH, D_MODEL), 0.02),
        'w_ple_gate': nrm(ks[25], (DEPTH, D_MODEL, D_MODEL), D_MODEL ** -0.5),
        'w_ple': nrm(ks[26], (DEPTH, PLE_DIM, D_MODEL), PLE_DIM ** -0.5),
        'g_final': 1.0 + nrm(ks[27], (D_MODEL,), 0.02),
    }


def reference(x_prompt, x_sample, cache_k, cache_v, cache_idx_k, state_hgrn, page_table,
              p_prompt, p_sample, g_ffn1, w1_gate, w1_up, w1_down, g_mix, w_in, hgrn_lb_logits,
              g_hgrn_out, w_branch_a, w_branch_b, w_out, g_ffn2, w2_gate, w2_up, w2_down,
              g_ple, w_ple_gate, w_ple, g_final):
    prm = dict(g_ffn1=g_ffn1, w1_gate=w1_gate, w1_up=w1_up, w1_down=w1_down, g_mix=g_mix,
               w_in=w_in, hgrn_lb_logits=hgrn_lb_logits, g_hgrn_out=g_hgrn_out,
               w_branch_a=w_branch_a, w_branch_b=w_branch_b, w_out=w_out, g_ffn2=g_ffn2,
               w2_gate=w2_gate, w2_up=w2_up, w2_down=w2_down, g_ple=g_ple,
               w_ple_gate=w_ple_gate, w_ple=w_ple, g_final=g_final)

    bp, tp = x_prompt.shape[:2]

    def hgrn_prompt(i, q, k, v, lf):
        s0 = jnp.zeros((bp, HGRN_HEADS, HGRN_DK, HGRN_DV), jnp.float32)
        return hgrn2_recurrence(q, k, v, lf, s0, math.gcd(tp, HGRN_CHUNK))

    def attn_prompt(i, q, k, v, iq, ik, iw):
        return prompt_attention(q, k, v, iq, ik, iw)

    y_prompt, k_prompt, v_prompt, idx_k_prompt, hgrn_state_prompt = trunk(
        x_prompt, p_prompt, jnp.arange(tp), hgrn_prompt, attn_prompt, prm)

    ts = x_sample.shape[1]
    past_len = page_table.shape[1] * PAGE_SIZE

    def hgrn_sample(i, q, k, v, lf):
        return hgrn2_recurrence(q, k, v, lf, state_hgrn[i].astype(jnp.float32), math.gcd(ts, HGRN_CHUNK))

    def attn_sample(i, q, k, v, iq, ik, iw):
        return sample_attention(q, k, v, iq, ik, iw, cache_k[i], cache_v[i], cache_idx_k[i], page_table)

    y_sample, k_sample, v_sample, idx_k_sample, hgrn_state_sample = trunk(
        x_sample, p_sample, past_len + jnp.arange(ts), hgrn_sample, attn_sample, prm)

    return (y_prompt, y_sample, k_prompt, v_prompt, idx_k_prompt, hgrn_state_prompt,
            k_sample, v_sample, idx_k_sample, hgrn_state_sample)
```

```python
import functools

import numpy as np
import jax
import jax.numpy as jnp
from jax import lax
from jax.experimental import pallas as pl
from jax.experimental.pallas import tpu as pltpu

F32 = jnp.float32
BF16 = jnp.bfloat16
I32 = jnp.int32

NORM_EPS = 1e-6
ROPE_THETA = 10000.0
HGRN_HEADS = 4
HGRN_DK = 128
HGRN_DV = 128
ATTN_HEADS = 8
HEAD_DIM = 64
IDX_HEADS = 8
IDX_DIM = 64
MAX_TOPK = 256
PAGE_SIZE = 128

LANES = 128
INT_MIN = -2147483648
INT_MAX = 2147483647
VMEM_LIMIT = 56 * 1024 * 1024
TOKEN_TILE = 512
FF_CHUNK = 256
HGRN_CHUNK = 128
KEY_CHUNK = 512
Q_TILE = 128

NT_DIMS = (((1,), (1,)), ((), ()))
TN_DIMS = (((0,), (0,)), ((), ()))


def _cparams(*sem):
    return pltpu.CompilerParams(dimension_semantics=sem, vmem_limit_bytes=VMEM_LIMIT)


def _const_spec(shape):
    nd = len(shape)
    return pl.BlockSpec(shape, lambda *_: (0,) * nd, pipeline_mode=pl.Buffered(1))


def _sigmoid(x):
    return 1.0 / (1.0 + jnp.exp(-x))


def _rmsnorm(x, g):
    return (x * lax.rsqrt(jnp.mean(x * x, axis=-1, keepdims=True) + NORM_EPS)) * g


def _dot(a, b):
    return jnp.dot(a, b, preferred_element_type=F32)


def _dot_nt(a, b):
    return lax.dot_general(a, b, NT_DIMS, preferred_element_type=F32)


def _ffn_core(x, g_ref, wg_ref, wu_ref, wd_ref, hb_scr, acc_scr):
    hb_scr[...] = _rmsnorm(x, g_ref[...]).astype(BF16)
    acc_scr[...] = jnp.zeros_like(acc_scr)

    def body(c, carry):
        hb = hb_scr[...]
        a = _dot(hb, wg_ref[c])
        u = _dot(hb, wu_ref[c])
        m = (a * _sigmoid(a)) * u
        acc_scr[...] += _dot(m.astype(BF16), wd_ref[c])
        return carry

    lax.fori_loop(0, wg_ref.shape[0], body, 0)
    return x + 0.5 * acc_scr[...]


def _ffn_body(x_ref, g_ref, wg_ref, wu_ref, wd_ref, o_ref, hb_scr, acc_scr):
    o_ref[...] = _ffn_core(x_ref[...], g_ref, wg_ref, wu_ref, wd_ref, hb_scr, acc_scr)


def _ffn(x, g, wg, wu, wd):
    n, d = x.shape
    tm = min(TOKEN_TILE, n)
    return pl.pallas_call(
        _ffn_body,
        grid=(n // tm,),
        in_specs=[pl.BlockSpec((tm, d), lambda i: (i, 0)), _const_spec(g.shape),
                  _const_spec(wg.shape), _const_spec(wu.shape), _const_spec(wd.shape)],
        out_specs=pl.BlockSpec((tm, d), lambda i: (i, 0)),
        out_shape=jax.ShapeDtypeStruct((n, d), F32),
        scratch_shapes=[pltpu.VMEM((tm, d), BF16), pltpu.VMEM((tm, d), F32)],
        compiler_params=_cparams("parallel"),
        name="ffn",
    )(x, g, wg, wu, wd)


def _ffn_ple_body(x_ref, p_ref, g_ref, wg_ref, wu_ref, wd_ref, gple_ref, wpg_ref, wple_ref, gfin_ref,
                  o_ref, hb_scr, acc_scr):
    x = _ffn_core(x_ref[...], g_ref, wg_ref, wu_ref, wd_ref, hb_scr, acc_scr)
    gate = _sigmoid(_dot(_rmsnorm(x, gple_ref[...]).astype(BF16), wpg_ref[...]))
    x = x + gate * _dot(p_ref[...].astype(BF16), wple_ref[...])
    o_ref[...] = _rmsnorm(x, gfin_ref[...])


def _ffn_ple(x, p, g, wg, wu, wd, gple, wpg, wple, gfin):
    n, d = x.shape
    tm = min(TOKEN_TILE, n)
    consts = (g, wg, wu, wd, gple, wpg, wple, gfin)
    return pl.pallas_call(
        _ffn_ple_body,
        grid=(n // tm,),
        in_specs=[pl.BlockSpec((tm, d), lambda i: (i, 0)), pl.BlockSpec((tm, p.shape[1]), lambda i: (i, 0))]
        + [_const_spec(c.shape) for c in consts],
        out_specs=pl.BlockSpec((tm, d), lambda i: (i, 0)),
        out_shape=jax.ShapeDtypeStruct((n, d), F32),
        scratch_shapes=[pltpu.VMEM((tm, d), BF16), pltpu.VMEM((tm, d), F32)],
        compiler_params=_cparams("parallel"),
        name="ffn_ple",
    )(x, p, *consts)


ROPE_W = (ATTN_HEADS + IDX_HEADS + 2) * HEAD_DIM
ATT_W = ROPE_W + LANES


def _proj_body(x_ref, g_ref, wh_ref, watt_ref, wgate_ref, cos_ref, sin_ref,
               zh_ref, q_ref, iq_ref, k_ref, ik_ref, v_ref, kb_ref, ikb_ref, vb_ref, iw_ref, gates_ref):
    hb = _rmsnorm(x_ref[...], g_ref[...]).astype(BF16)
    zh_ref[...] = _dot(hb, wh_ref[...])
    gates_ref[...] = _sigmoid(_dot(hb, wgate_ref[...])).astype(BF16)
    za = _dot(hb, watt_ref[...])
    zr = za[:, :ROPE_W]
    reps = ROPE_W // LANES
    cos = jnp.concatenate([cos_ref[...]] * reps, axis=1)
    sin = jnp.concatenate([sin_ref[...]] * reps, axis=1)
    lane = lax.broadcasted_iota(I32, zr.shape, 1)
    half = HEAD_DIM // 2
    swapped = jnp.where((lane & (HEAD_DIM - 1)) < half,
                        pltpu.roll(zr, ROPE_W - half, 1), pltpu.roll(zr, half, 1))
    rot = zr * cos + swapped * sin
    qs = HEAD_DIM ** -0.5
    for h in range(ATTN_HEADS):
        q_ref[h] = (rot[:, h * HEAD_DIM:(h + 1) * HEAD_DIM] * qs).astype(BF16)
    o0 = ATTN_HEADS * HEAD_DIM
    for h in range(IDX_HEADS):
        iq_ref[h] = rot[:, o0 + h * IDX_DIM:o0 + (h + 1) * IDX_DIM].astype(BF16)
    o1 = o0 + IDX_HEADS * IDX_DIM
    kk = rot[:, o1:o1 + HEAD_DIM]
    ik = rot[:, o1 + HEAD_DIM:o1 + HEAD_DIM + IDX_DIM]
    vv = za[:, ROPE_W:ROPE_W + HEAD_DIM]
    k_ref[...] = kk
    ik_ref[...] = ik
    v_ref[...] = vv
    kb_ref[...] = kk.astype(BF16)
    ikb_ref[...] = ik.astype(BF16)
    vb_ref[...] = vv.astype(BF16)
    iw_ref[...] = za[:, ROPE_W + HEAD_DIM:ROPE_W + HEAD_DIM + IDX_HEADS] * ((IDX_HEADS ** -0.5) * (IDX_DIM ** -0.5))


def _proj(x, g, wh, watt, wgate, cos_t, sin_t):
    n, d = x.shape
    tm = min(TOKEN_TILE, n)
    period = cos_t.shape[0] // tm
    tok = lambda w: pl.BlockSpec((tm, w), lambda i: (i, 0))
    headed = pl.BlockSpec((ATTN_HEADS, tm, HEAD_DIM), lambda i: (0, i, 0))
    rope_spec = pl.BlockSpec((tm, LANES), lambda i: (i % period, 0))
    sd = jax.ShapeDtypeStruct
    return pl.pallas_call(
        _proj_body,
        grid=(n // tm,),
        in_specs=[tok(d), _const_spec(g.shape), _const_spec(wh.shape), _const_spec(watt.shape),
                  _const_spec(wgate.shape), rope_spec, rope_spec],
        out_specs=[tok(wh.shape[1]), headed, headed, tok(HEAD_DIM), tok(IDX_DIM), tok(HEAD_DIM),
                   tok(HEAD_DIM), tok(IDX_DIM), tok(HEAD_DIM), tok(IDX_HEADS), tok(wgate.shape[1])],
        out_shape=[sd((n, wh.shape[1]), F32), sd((ATTN_HEADS, n, HEAD_DIM), BF16),
                   sd((IDX_HEADS, n, IDX_DIM), BF16), sd((n, HEAD_DIM), F32), sd((n, IDX_DIM), F32),
                   sd((n, HEAD_DIM), F32), sd((n, HEAD_DIM), BF16), sd((n, IDX_DIM), BF16),
                   sd((n, HEAD_DIM), BF16), sd((n, IDX_HEADS), F32), sd((n, wgate.shape[1]), BF16)],
        compiler_params=_cparams("parallel"),
        name="proj_in",
    )(x, g, wh, watt, wgate, cos_t, sin_t)


def _hgrn_tables(c):
    nlev = int(np.log2(c))
    t = np.arange(c)[:, None]
    r = np.arange(c)[None, :]
    blocks = [(r <= t), (r > t)]
    masks = []
    for lev in range(nlev):
        m = c >> (lev + 1)
        mid = (t // (2 * m)) * (2 * m) + m
        is_q = t >= mid
        blocks.append(np.where(is_q, (r >= mid) & (r <= t), (r > t) & (r < mid)))
        s = r
        masks.append(((t // (2 * m)) == (s // (2 * m))) & is_q & ((s % (2 * m)) < m))
    dall = np.concatenate(blocks, axis=0).astype(np.float32)
    return jnp.asarray(dall, BF16), jnp.asarray(np.stack(masks).astype(np.float32)), nlev


def _hgrn_body(zq_ref, zf_ref, zi_ref, zg_ref, s0_ref, lb_ref, gout_ref, dall_ref, lmask_ref,
               o_ref, s_out_ref, st_scr, *, t_len, chunk, nlev):
    c = chunk
    st_scr[...] = s0_ref[0, 0].T
    lb = lb_ref[0]
    gout = gout_ref[0]
    n_valid = min(t_len, c)

    def load(ref, r0):
        x = ref[0, pl.ds(r0, n_valid), :]
        if n_valid < c:
            x = jnp.concatenate([x, jnp.zeros((c - n_valid, x.shape[1]), x.dtype)], axis=0)
        return x

    def body(ci, carry):
        r0 = pl.multiple_of(ci * n_valid, n_valid)
        zq, zf, zi, zg = (load(r, r0) for r in (zq_ref, zf_ref, zi_ref, zg_ref))
        f = lb + (1.0 - lb) * _sigmoid(zf)
        logf = jnp.log(f)
        k = 1.0 - f
        q = zq * _sigmoid(zq)
        v = zi
        if n_valid < c:
            rv = lax.broadcasted_iota(I32, (c, 1), 0) < n_valid
            logf = jnp.where(rv, logf, 0.0)
            k = jnp.where(rv, k, 0.0)
            q = jnp.where(rv, q, 0.0)
            v = jnp.where(rv, v, 0.0)
        lf_hi = logf.astype(BF16)
        lf_lo = (logf - lf_hi.astype(F32)).astype(BF16)
        dall = dall_ref[...]
        args = _dot(dall, lf_hi) + _dot(dall, lf_lo)
        e = jnp.exp(args)
        a = jnp.zeros((c, c), F32)
        for lev in range(nlev):
            el = e[(2 + lev) * c:(3 + lev) * c]
            a = a + _dot_nt((q * el).astype(BF16), (k * el).astype(BF16)) * lmask_ref[lev]
        vb = v.astype(BF16)
        st = st_scr[...]
        o = (_dot(a.astype(BF16), vb) + _dot_nt((q * e[0:c]).astype(BF16), st.astype(BF16))
             + jnp.sum(q * k, axis=1, keepdims=True) * v)
        st_scr[...] = st * e[c - 1:c] + lax.dot_general(vb, (k * e[c:2 * c]).astype(BF16), TN_DIMS,
                                                        preferred_element_type=F32)
        y = _rmsnorm(o, gout) * (zg * _sigmoid(zg))
        o_ref[0, pl.ds(r0, n_valid), :] = y[:n_valid].astype(o_ref.dtype)
        return carry

    lax.fori_loop(0, max(t_len // c, 1), body, 0)
    s_out_ref[0, 0] = st_scr[...].T


def _hgrn(zh, s0, lb, gout):
    b, t, _ = zh.shape
    c = HGRN_CHUNK
    dall, lmask, nlev = _hgrn_tables(c)
    col = lambda j: pl.BlockSpec((1, t, HGRN_DK), lambda bi, h: (bi, 0, j * HGRN_HEADS + h))
    per_head = pl.BlockSpec((1, 1, HGRN_DK), lambda bi, h: (h, 0, 0))
    state = pl.BlockSpec((1, 1, HGRN_DK, HGRN_DV), lambda bi, h: (bi, h, 0, 0))
    return pl.pallas_call(
        functools.partial(_hgrn_body, t_len=t, chunk=c, nlev=nlev),
        grid=(b, HGRN_HEADS),
        in_specs=[col(0), col(1), col(2), col(3), state, per_head, per_head,
                  _const_spec(dall.shape), _const_spec(lmask.shape)],
        out_specs=[pl.BlockSpec((1, t, HGRN_DV), lambda bi, h: (bi, 0, h)), state],
        out_shape=[jax.ShapeDtypeStruct((b, t, HGRN_HEADS * HGRN_DV), BF16),
                   jax.ShapeDtypeStruct(s0.shape, F32)],
        scratch_shapes=[pltpu.VMEM((HGRN_DV, HGRN_DK), F32)],
        compiler_params=_cparams("parallel", "parallel"),
        name="hgrn",
    )(zh, zh, zh, zh, s0, lb, gout, dall, lmask)


def _sort_key(score):
    score = jnp.where(score == 0.0, 0.0, score)
    bits = lax.bitcast_convert_type(score, I32)
    return bits ^ ((bits >> 31) & INT_MAX)


def _kth_largest(count_ge, rows, topk):
    kf = float(topk)
    t0 = jnp.where(count_ge(jnp.zeros((rows, 1), I32)) >= kf, 0, INT_MIN).astype(I32)

    def body(i, t):
        cand = t | jnp.left_shift(jnp.int32(1), 30 - i)
        return jnp.where(count_ge(cand) >= kf, cand, t)

    return lax.fori_loop(0, 31, body, t0)


def _tie_cut(count_eq_upto, need, nbits):
    def body(i, p):
        cand = p + jnp.left_shift(jnp.int32(1), nbits - 1 - i)
        return jnp.where(count_eq_upto(cand - 1) < need, cand, p)

    return lax.fori_loop(0, nbits, body, jnp.zeros_like(need).astype(I32))


def _attn_prompt_body(q_ref, iq_ref, iw_ref, k_ref, v_ref, ik_ref, o_ref,
                      key_scr, m_scr, l_scr, acc_scr, *, topk, kc):
    qi = pl.program_id(1)
    nh, tq, dh = q_ref.shape
    nkc = (qi * tq + tq - 1) // kc + 1
    row_pos = qi * tq + lax.broadcasted_iota(I32, (tq, kc), 0)
    lane = lax.broadcasted_iota(I32, (tq, kc), 1)
    iw = iw_ref[...]
    ncol = kc // LANES

    def scores(c, carry):
        c0 = pl.multiple_of(c * kc, kc)
        ikc = ik_ref[pl.ds(c0, kc), :]
        acc = jnp.zeros((tq, kc), F32)
        for h in range(nh):
            acc = acc + iw[:, h:h + 1] * jnp.maximum(_dot_nt(iq_ref[h], ikc), 0.0)
        key_scr[c] = jnp.where(c0 + lane <= row_pos, _sort_key(acc), INT_MIN)
        return carry

    lax.fori_loop(0, nkc, scores, 0)

    def count(pred):
        def body(c, acc):
            kch = key_scr[c]
            hit = pred(kch, c * kc + lane)
            for j in range(ncol):
                acc = acc + jnp.where(hit[:, j * LANES:(j + 1) * LANES], 1.0, 0.0)
            return acc
        acc = lax.fori_loop(0, nkc, body, jnp.zeros((tq, LANES), F32))
        return jnp.sum(acc, axis=1, keepdims=True)

    thr = _kth_largest(lambda cand: count(lambda kch, pos: kch >= cand), tq, topk)
    n_gt = count(lambda kch, pos: kch > thr)
    n_eq = count(lambda kch, pos: kch == thr)
    need = float(topk) - n_gt
    excess = (thr > INT_MIN) & (n_eq > need)

    def tie_path():
        cut = _tie_cut(lambda j: count(lambda kch, pos: (kch == thr) & (pos <= j)), need,
                       int(k_ref.shape[0] - 1).bit_length())
        return jnp.where(excess, cut, INT_MAX)

    cut = lax.cond(jnp.max(jnp.where(excess, 1.0, 0.0)) > 0.0, tie_path,
                   lambda: jnp.full((tq, 1), INT_MAX, I32))

    m_scr[...] = jnp.full(m_scr.shape, -jnp.inf, F32)
    l_scr[...] = jnp.zeros(l_scr.shape, F32)
    acc_scr[...] = jnp.zeros(acc_scr.shape, F32)

    def attend(c, carry):
        c0 = pl.multiple_of(c * kc, kc)
        kch = key_scr[c]
        pos = c0 + lane
        sel = ((kch > thr) | ((kch == thr) & (pos <= cut))) & (pos <= row_pos)
        kb = k_ref[pl.ds(c0, kc), :]
        vb = v_ref[pl.ds(c0, kc), :]
        for h in range(nh):
            lg = jnp.where(sel, _dot_nt(q_ref[h], kb), -jnp.inf)
            m_old = m_scr[h]
            m_new = jnp.maximum(m_old, jnp.max(lg, axis=1, keepdims=True))
            m_safe = jnp.where(m_new == -jnp.inf, 0.0, m_new)
            p = jnp.exp(lg - m_safe)
            alpha = jnp.exp(m_old - m_safe)
            l_scr[h] = alpha * l_scr[h] + jnp.sum(p, axis=1, keepdims=True)
            acc_scr[h] = alpha * acc_scr[h] + _dot(p.astype(BF16), vb)
            m_scr[h] = m_new
        return carry

    lax.fori_loop(0, nkc, attend, 0)
    o_ref[...] = jnp.concatenate([acc_scr[h] / l_scr[h] for h in range(nh)], axis=1).astype(o_ref.dtype)


def _attn_prompt(q, iq, iw, kb, vb, ikb, batch, seq):
    nh, n, dh = q.shape
    tq = min(Q_TILE, seq)
    kc = min(KEY_CHUNK, seq)
    nq = seq // tq
    topk = min(MAX_TOPK, seq // 4)
    headed = pl.BlockSpec((nh, tq, dh), lambda b, i: (0, b * nq + i, 0))
    keys = pl.BlockSpec((seq, dh), lambda b, i: (b, 0))
    return pl.pallas_call(
        functools.partial(_attn_prompt_body, topk=topk, kc=kc),
        grid=(batch, nq),
        in_specs=[headed, headed, pl.BlockSpec((tq, nh), lambda b, i: (b * nq + i, 0)), keys, keys, keys],
        out_specs=pl.BlockSpec((tq, nh * dh), lambda b, i: (b * nq + i, 0)),
        out_shape=jax.ShapeDtypeStruct((n, nh * dh), BF16),
        scratch_shapes=[pltpu.VMEM((seq // kc, tq, kc), I32), pltpu.VMEM((nh, tq, 1), F32),
                        pltpu.VMEM((nh, tq, 1), F32), pltpu.VMEM((nh, tq, dh), F32)],
        compiler_params=_cparams("parallel", "parallel"),
        name="attn_prompt",
    )(q, iq, iw, kb, vb, ikb)


def _attn_sample_body(pt_ref, q_ref, iq_ref, iw_ref, kn_ref, vn_ref, ikn_ref, ck_hbm, cv_hbm, cik_hbm,
                      o_ref, kbuf, vbuf, ikbuf, sems, *, topk, n_pages):
    b = pl.program_id(0)
    ts = kn_ref.shape[0]
    past = n_pages * PAGE_SIZE
    ltot = kbuf.shape[0]
    nh = q_ref.shape[1] // ts

    def page_copy(src, dst, j, s):
        return pltpu.make_async_copy(src.at[pt_ref[b, j]], dst.at[pl.ds(j * PAGE_SIZE, PAGE_SIZE), :],
                                     sems.at[s])

    pairs = ((cik_hbm, ikbuf, 0), (ck_hbm, kbuf, 1), (cv_hbm, vbuf, 2))
    for j in range(n_pages):
        for src, dst, s in pairs:
            page_copy(src, dst, j, s).start()
    tail = jnp.zeros((ltot - past - ts, kbuf.shape[1]), F32)
    for new, dst in ((ikn_ref, ikbuf), (kn_ref, kbuf), (vn_ref, vbuf)):
        dst[pl.ds(past, ts), :] = new[...]
        dst[pl.ds(past + ts, ltot - past - ts), :] = tail
    for j in range(n_pages):
        for src, dst, s in pairs:
            page_copy(src, dst, j, s).wait()

    pos = lax.broadcasted_iota(I32, (ts, ltot), 1)
    causal = pos <= past + lax.broadcasted_iota(I32, (ts, ltot), 0)
    s = jnp.maximum(_dot_nt(iq_ref[0], ikbuf[...].astype(BF16)), 0.0) * iw_ref[0]
    score = jnp.sum(s.reshape(nh, ts, ltot), axis=0)
    key = jnp.where(causal, _sort_key(score), INT_MIN)

    def count(hit):
        return jnp.sum(jnp.where(hit, 1.0, 0.0), axis=1, keepdims=True)

    thr = _kth_largest(lambda cand: count(key >= cand), ts, topk)
    need = float(topk) - count(key > thr)
    excess = (thr > INT_MIN) & (count(key == thr) > need)
    cut = jnp.where(excess,
                    _tie_cut(lambda j: count((key == thr) & (pos <= j)), need, int(ltot - 1).bit_length()),
                    INT_MAX)
    sel = ((key > thr) | ((key == thr) & (pos <= cut))) & causal

    lg = _dot_nt(q_ref[0], kbuf[...].astype(BF16)).reshape(nh, ts, ltot)
    lg = jnp.where(sel[None], lg, -jnp.inf)
    p = jnp.exp(lg - jnp.max(lg, axis=2, keepdims=True))
    denom = jnp.sum(p, axis=2, keepdims=True)
    o = _dot(p.reshape(nh * ts, ltot).astype(BF16), vbuf[...].astype(BF16)).reshape(nh, ts, -1) / denom
    o_ref[...] = jnp.concatenate([o[h] for h in range(nh)], axis=1).astype(o_ref.dtype)


def _attn_sample(q, iq, iw, k_new, v_new, ik_new, ck, cv, cik, page_table, dec_batch, ts):
    n_pages = page_table.shape[1]
    past = n_pages * PAGE_SIZE
    dh = q.shape[2]
    rows = q.shape[1]
    ltot = past + LANES
    topk = min(MAX_TOPK, (past + ts) // 4)
    per_b3 = lambda w: pl.BlockSpec((1, rows, w), lambda b, pt: (b, 0, 0))
    new = pl.BlockSpec((ts, dh), lambda b, pt: (b, 0))
    hbm = pl.BlockSpec(memory_space=pl.ANY)
    grid_spec = pltpu.PrefetchScalarGridSpec(
        num_scalar_prefetch=1,
        grid=(dec_batch,),
        in_specs=[per_b3(dh), per_b3(dh), per_b3(1), new, new, new, hbm, hbm, hbm],
        out_specs=pl.BlockSpec((ts, rows // ts * dh), lambda b, pt: (b, 0)),
        scratch_shapes=[pltpu.VMEM((ltot, dh), F32), pltpu.VMEM((ltot, dh), F32), pltpu.VMEM((ltot, dh), F32),
                        pltpu.SemaphoreType.DMA((3,))],
    )
    return pl.pallas_call(
        functools.partial(_attn_sample_body, topk=topk, n_pages=n_pages),
        grid_spec=grid_spec,
        out_shape=jax.ShapeDtypeStruct((dec_batch * ts, rows // ts * dh), BF16),
        compiler_params=_cparams("arbitrary"),
        name="attn_sample",
    )(page_table, q, iq, iw, k_new, v_new, ik_new, ck, cv, cik)


def _merge_body(x_ref, oa_ref, ob_ref, gates_ref, wa_ref, wb_ref, wo_ref, o_ref):
    d = x_ref.shape[1]
    gates = gates_ref[...].astype(F32)
    merged = gates[:, :d] * _dot(oa_ref[...], wa_ref[...]) + gates[:, d:] * _dot(ob_ref[...], wb_ref[...])
    o_ref[...] = x_ref[...] + _dot(merged.astype(BF16), wo_ref[...])


def _merge(x, oa, ob, gates, wa, wb, wo):
    n, d = x.shape
    tm = min(TOKEN_TILE, n)
    tok = lambda w: pl.BlockSpec((tm, w), lambda i: (i, 0))
    return pl.pallas_call(
        _merge_body,
        grid=(n // tm,),
        in_specs=[tok(d), tok(oa.shape[1]), tok(ob.shape[1]), tok(gates.shape[1]),
                  _const_spec(wa.shape), _const_spec(wb.shape), _const_spec(wo.shape)],
        out_specs=tok(d),
        out_shape=jax.ShapeDtypeStruct((n, d), F32),
        compiler_params=_cparams("parallel"),
        name="merge_out",
    )(x, oa, ob, gates, wa, wb, wo)


def _rope_tables(pos, rows):
    half = HEAD_DIM // 2
    inv = ROPE_THETA ** (-jnp.arange(half, dtype=F32) / half)
    ang = pos.astype(F32)[:, None] * inv[None, :]
    cos, sin = jnp.cos(ang), jnp.sin(ang)
    reps = LANES // HEAD_DIM
    cos_t = jnp.tile(jnp.concatenate([cos, cos], axis=1), (1, reps))
    sin_t = jnp.tile(jnp.concatenate([-sin, sin], axis=1), (1, reps))
    if cos_t.shape[0] < rows:
        cos_t = jnp.tile(cos_t, (rows // cos_t.shape[0], 1))
        sin_t = jnp.tile(sin_t, (rows // sin_t.shape[0], 1))
    return cos_t, sin_t


def _chunk_cols(w):
    d, f = w.shape
    return w.astype(BF16).reshape(d, f // FF_CHUNK, FF_CHUNK).transpose(1, 0, 2)


def _chunk_rows(w):
    f, d = w.shape
    return w.astype(BF16).reshape(f // FF_CHUNK, FF_CHUNK, d)


def _prepare(prm):
    i = 0
    w_in = prm["w_in"][i]
    hw = 4 * HGRN_HEADS * HGRN_DK
    o = hw
    cols = {}
    for name, width in (("qb", ATTN_HEADS * HEAD_DIM), ("kb", HEAD_DIM), ("vb", HEAD_DIM),
                        ("qi", IDX_HEADS * IDX_DIM), ("ki", IDX_DIM), ("wi", IDX_HEADS)):
        cols[name] = w_in[:, o:o + width]
        o += width
    pad = jnp.zeros((w_in.shape[0], ATT_W - ROPE_W - HEAD_DIM - IDX_HEADS), w_in.dtype)
    watt = jnp.concatenate([cols["qb"], cols["qi"], cols["kb"], cols["ki"], cols["vb"], cols["wi"], pad], axis=1)
    lb_all = jnp.cumsum(jax.nn.softmax(prm["hgrn_lb_logits"].astype(F32), axis=0), axis=0)
    row = lambda a: a.reshape(1, -1)
    return dict(
        g_ffn1=row(prm["g_ffn1"][i]), w1g=_chunk_cols(prm["w1_gate"][i]), w1u=_chunk_cols(prm["w1_up"][i]),
        w1d=_chunk_rows(prm["w1_down"][i]),
        g_mix=row(prm["g_mix"][i]), wh=w_in[:, :hw].astype(BF16), watt=watt.astype(BF16),
        wgate=w_in[:, o:].astype(BF16),
        lb=lb_all[i].reshape(HGRN_HEADS, 1, HGRN_DK), gout=prm["g_hgrn_out"][i].reshape(HGRN_HEADS, 1, HGRN_DV),
        wa=prm["w_branch_a"][i].astype(BF16), wb=prm["w_branch_b"][i].astype(BF16), wo=prm["w_out"][i].astype(BF16),
        g_ffn2=row(prm["g_ffn2"][i]), w2g=_chunk_cols(prm["w2_gate"][i]), w2u=_chunk_cols(prm["w2_up"][i]),
        w2d=_chunk_rows(prm["w2_down"][i]),
        g_ple=row(prm["g_ple"][i]), wpg=prm["w_ple_gate"][i].astype(BF16), wple=prm["w_ple"][i].astype(BF16),
        g_final=row(prm["g_final"]),
    )


def _trunk(x, p, pos, w, s0, attn_fn):
    b, t, d = x.shape
    n = b * t
    x0 = x.reshape(n, d)
    x1 = _ffn(x0, w["g_ffn1"], w["w1g"], w["w1u"], w["w1d"])
    cos_t, sin_t = _rope_tables(pos, min(TOKEN_TILE, n))
    zh, q, iq, k, ik, v, kb, ikb, vb, iw, gates = _proj(x1, w["g_mix"], w["wh"], w["watt"], w["wgate"],
                                                        cos_t, sin_t)
    oa, s_out = _hgrn(zh.reshape(b, t, -1), s0, w["lb"], w["gout"])
    ob = attn_fn(q, iq, iw, k, v, ik, kb, vb, ikb)
    x2 = _merge(x1, oa.reshape(n, -1), ob, gates, w["wa"], w["wb"], w["wo"])
    y = _ffn_ple(x2, p.reshape(n, -1), w["g_ffn2"], w["w2g"], w["w2u"], w["w2d"],
                 w["g_ple"], w["wpg"], w["wple"], w["g_final"])
    shape4 = lambda a: a.reshape(1, b, t, -1)
    return y.reshape(b, t, d), shape4(k), shape4(v), shape4(ik), s_out[None]


def kernel(x_prompt, x_sample, cache_k, cache_v, cache_idx_k, state_hgrn, page_table, p_prompt, p_sample,
           g_ffn1, w1_gate, w1_up, w1_down, g_mix, w_in, hgrn_lb_logits, g_hgrn_out, w_branch_a, w_branch_b,
           w_out, g_ffn2, w2_gate, w2_up, w2_down, g_ple, w_ple_gate, w_ple, g_final):
    w = _prepare(dict(g_ffn1=g_ffn1, w1_gate=w1_gate, w1_up=w1_up, w1_down=w1_down, g_mix=g_mix, w_in=w_in,
                      hgrn_lb_logits=hgrn_lb_logits, g_hgrn_out=g_hgrn_out, w_branch_a=w_branch_a,
                      w_branch_b=w_branch_b, w_out=w_out, g_ffn2=g_ffn2, w2_gate=w2_gate, w2_up=w2_up,
                      w2_down=w2_down, g_ple=g_ple, w_ple_gate=w_ple_gate, w_ple=w_ple, g_final=g_final))
    bp, tp, _ = x_prompt.shape
    bd, ts, _ = x_sample.shape
    past = page_table.shape[1] * PAGE_SIZE

    def attn_prompt(q, iq, iw, k, v, ik, kb, vb, ikb):
        return _attn_prompt(q, iq, iw, kb, vb, ikb, bp, tp)

    def attn_sample(q, iq, iw, k, v, ik, kb, vb, ikb):
        per_b = lambda a: a.reshape(a.shape[0], bd, ts, a.shape[2]).transpose(1, 0, 2, 3).reshape(bd, -1, a.shape[2])
        iw_rows = iw.reshape(bd, ts, -1).transpose(0, 2, 1).reshape(bd, -1, 1)
        return _attn_sample(per_b(q), per_b(iq), iw_rows, k, v, ik, cache_k[0], cache_v[0], cache_idx_k[0],
                            page_table, bd, ts)

    zero_state = jnp.zeros((bp, HGRN_HEADS, HGRN_DK, HGRN_DV), F32)
    y_p, k_p, v_p, ik_p, s_p = _trunk(x_prompt, p_prompt[0], jnp.arange(tp), w, zero_state, attn_prompt)
    y_s, k_s, v_s, ik_s, s_s = _trunk(x_sample, p_sample[0], past + jnp.arange(ts), w,
                                      state_hgrn[0].astype(F32), attn_sample)
    return (y_p, y_s, k_p, v_p, ik_p, s_p, k_s, v_s, ik_s, s_s)
```
